```python
import math
import jax
import jax.numpy as jnp
from jax import lax
import numpy as np

D_MODEL = 1024
BATCH = 4
SEQ = 4096
DEPTH = 4
DEC_BATCH = 32
DEC_SEQ = 1
PAST_LEN = 8192
PAGE_SIZE = 128

N_EVEN = (DEPTH + 1) // 2
N_ODD = DEPTH // 2
HEAD_DIM = 64
ROT_DIM = HEAD_DIM // 4
ROPE_THETA = 500000.0
Q_BLOCK = 128
EPS = 1e-6
HALF = D_MODEL // 2

FOX_HEADS = HALF // HEAD_DIM
FOX_KV_HEADS = FOX_HEADS // 2
FOX_WIDTH = FOX_HEADS * HEAD_DIM
FOX_F_BIAS = 3.0
DN_DK = 128
DN_DV = 128
DN_HEADS = HALF // DN_DV
DN_WIDTH = DN_HEADS * DN_DV
DN_CONV_DIM = DN_HEADS * (2 * DN_DK + DN_DV)
CONV_W = 4
DN_CHUNK = 64
DSA_HEADS = HALF // HEAD_DIM
DSA_KV_HEADS = DSA_HEADS // 2
DSA_WIDTH = DSA_HEADS * HEAD_DIM
IDX_HEADS = 4
IDX_DIM = 64
TOPK_MAX = 256
DIFF_HEADS = HALF // (2 * HEAD_DIM)
DIFF_KV_HEADS = DIFF_HEADS // 2
DIFF_WIDTH = DIFF_HEADS * 2 * HEAD_DIM

EVEN_SIZES = (FOX_HEADS * HEAD_DIM, FOX_KV_HEADS * HEAD_DIM, FOX_KV_HEADS * HEAD_DIM, FOX_HEADS, FOX_WIDTH,
              DN_CONV_DIM, DN_HEADS, DN_HEADS, DN_WIDTH)
ODD_SIZES = (DSA_HEADS * HEAD_DIM, DSA_KV_HEADS * HEAD_DIM, DSA_KV_HEADS * HEAD_DIM, IDX_HEADS * IDX_DIM, IDX_DIM,
             IDX_HEADS, DSA_WIDTH, DIFF_HEADS * 2 * HEAD_DIM, DIFF_KV_HEADS * 2 * HEAD_DIM,
             DIFF_KV_HEADS * 2 * HEAD_DIM, DIFF_WIDTH)
EVEN_IN = sum(EVEN_SIZES)
ODD_IN = sum(ODD_SIZES)
MIX_WIDTH = FOX_WIDTH + DN_WIDTH
F32 = jnp.float32

kernel_name = 'hybrid_fox_gdn_dsa_diff_step'


def _split(a, sizes):
    return jnp.split(a, np.cumsum(sizes)[:-1].tolist(), axis=-1)


def _rms(x, g):
    xf = x.astype(F32)
    y = xf * lax.rsqrt(jnp.mean(xf * xf, axis=-1, keepdims=True) + EPS)
    return (y * g.astype(F32)).astype(x.dtype)


def _l2norm(x):
    return x * lax.rsqrt(jnp.sum(x * x, axis=-1, keepdims=True) + EPS)


def _rope(x, pos):
    half = ROT_DIM // 2
    inv_freq = 1.0 / (ROPE_THETA ** (jnp.arange(half, dtype=F32) * 2.0 / ROT_DIM))
    ang = pos.astype(F32)[:, None] * inv_freq[None, :]
    cos = jnp.cos(ang)[None, :, None, :]
    sin = jnp.sin(ang)[None, :, None, :]
    xr = x[..., :ROT_DIM].astype(F32)
    x1, x2 = xr[..., :half], xr[..., half:]
    rot = jnp.concatenate([x1 * cos - x2 * sin, x2 * cos + x1 * sin], axis=-1)
    return jnp.concatenate([rot.astype(x.dtype), x[..., ROT_DIM:]], axis=-1)


def _gather_pages(pool, page_table):
    g = pool[page_table]
    return g.reshape((page_table.shape[0], page_table.shape[1] * PAGE_SIZE) + pool.shape[2:])


def _gather_seq(arr, idx):
    return jax.vmap(lambda a, i: a[i])(arr, idx)


def _gather_rows_paged(pool, page_table, new_rows, idx):
    past_len = page_table.shape[1] * PAGE_SIZE
    l_new = new_rows.shape[1]
    in_past = idx < past_len
    pi = jnp.minimum(idx, past_len - 1)
    phys = jax.vmap(lambda pt, i: pt[i])(page_table, pi // PAGE_SIZE)
    rows_past = pool[phys, pi % PAGE_SIZE]
    rows_new = _gather_seq(new_rows, jnp.clip(idx - past_len, 0, l_new - 1))
    sel = in_past.reshape(in_past.shape + (1,) * (rows_past.ndim - 3))
    return jnp.where(sel, rows_past, rows_new)


def _sweep(fn, args, lq):
    qb = Q_BLOCK if lq % Q_BLOCK == 0 else lq
    nb = lq // qb

    def split(a):
        return jnp.moveaxis(a.reshape((a.shape[0], nb, qb) + a.shape[2:]), 1, 0)

    out = lax.map(fn, tuple(split(a) for a in args))
    out = jnp.moveaxis(out, 0, 1)
    return out.reshape((out.shape[0], lq) + out.shape[3:])


def _fox_attend(q, k, v, cq, ck, q_pos, k_pos):
    b, lq, h, d = q.shape
    hkv = k.shape[2]
    g = h // hkv
    scale = d ** -0.5
    kf, vf = k.astype(F32), v.astype(F32)
    ck_t = ck.reshape(b, -1, hkv, g).transpose(0, 2, 3, 1)[..., None, :]

    def block(args):
        qb, cqb, pb = args
        qg = qb.astype(F32).reshape(b, -1, hkv, g, d)
        s = jnp.einsum('bqkgd,bskd->bkgqs', qg, kf) * scale
        s = s + cqb.reshape(b, -1, hkv, g).transpose(0, 2, 3, 1)[..., None] - ck_t
        mask = k_pos[None, :] <= pb[0][:, None]
        p = jax.nn.softmax(jnp.where(mask, s, -jnp.inf), axis=-1)
        o = jnp.einsum('bkgqs,bskd->bqkgd', p, vf)
        return o.reshape(b, -1, h, d)

    return _sweep(block, (q, cq, q_pos[None, :]), lq).astype(q.dtype)


def _dsa_attend(q, qi, wi, q_pos, ki_all, k_pos, gather_kv):
    b, lq, h, d = q.shape
    lk = ki_all.shape[1]
    topk = min(TOPK_MAX, lk // 4)
    scale = d ** -0.5
    iscale = IDX_DIM ** -0.5
    kif = ki_all.astype(F32)

    def block(args):
        qb, qib, wib, pb = args
        sc = jax.nn.relu(jnp.einsum('bqhd,bsd->bqhs', qib.astype(F32), kif) * iscale)
        iscore = jnp.einsum('bqhs,bqh->bqs', sc, wib.astype(F32))
        adm = k_pos[None, None, :] <= pb[0][None, :, None]
        iscore = jnp.where(adm, iscore, -jnp.inf)
        top_s, top_i = lax.top_k(iscore, topk)
        valid = top_s > -jnp.inf
        kv = gather_kv(top_i)
        ks = kv[:, :, :, 0].astype(F32)
        vs = kv[:, :, :, 1].astype(F32)
        hkv = ks.shape[3]
        qg = qb.astype(F32).reshape(b, -1, hkv, h // hkv, d)
        s = jnp.einsum('bqkgd,bqnkd->bqkgn', qg, ks) * scale
        s = jnp.where(valid[:, :, None, None, :], s, -jnp.inf)
        p = jax.nn.softmax(s, axis=-1)
        o = jnp.einsum('bqkgn,bqnkd->bqkgd', p, vs)
        return o.reshape(b, -1, h, d)

    return _sweep(block, (q, qi, wi, q_pos[None, :]), lq).astype(q.dtype)


def _diff_attend(q, k, v, lam, q_pos, k_pos):
    b, lq, h, _, d = q.shape
    hkv = k.shape[2]
    g = h // hkv
    scale = d ** -0.5
    kf, vf = k.astype(F32), v.astype(F32)

    def block(args):
        qb, pb = args
        qg = qb.astype(F32).reshape(b, -1, hkv, g, 2, d)
        s = jnp.einsum('bqkgcd,bskcd->cbkgqs', qg, kf) * scale
        mask = k_pos[None, :] <= pb[0][:, None]
        p = jax.nn.softmax(jnp.where(mask, s, -jnp.inf), axis=-1)
        a = p[0] - lam * p[1]
        o = jnp.einsum('bkgqs,bske->bqkge', a, vf)
        return o.reshape(b, -1, h, 2 * d)

    return _sweep(block, (q, q_pos[None, :]), lq).astype(q.dtype)


def _gated_delta(q, k, v, g, beta, s0):
    b, l, h, _ = q.shape
    dv = v.shape[-1]
    c = DN_CHUNK if l % DN_CHUNK == 0 else l
    n = l // c

    def chunks(a):
        a = a.reshape((b, n, c) + a.shape[2:])
        return jnp.moveaxis(jnp.moveaxis(a, 1, 0), 3, 2)

    qc, kc, vc, gc, bc = (chunks(a) for a in (q, k, v, g, beta))
    gcum = jnp.cumsum(gc, axis=-1)
    incl = jnp.tril(jnp.ones((c, c), bool))
    strict = jnp.tril(jnp.ones((c, c), bool), -1)
    decay = jnp.exp(jnp.where(incl, gcum[..., :, None] - gcum[..., None, :], -jnp.inf))
    kb = kc * bc[..., None]
    a_mat = jnp.where(strict, jnp.einsum('nbhid,nbhjd->nbhij', kb, kc) * decay, 0.0)
    eye = jnp.eye(c, dtype=F32)
    t_mat = lax.linalg.triangular_solve(a_mat + eye, jnp.broadcast_to(eye, a_mat.shape),
                                        left_side=True, lower=True, unit_diagonal=True)
    u = t_mat @ (vc * bc[..., None])
    w = t_mat @ (kb * jnp.exp(gcum)[..., None])
    qk = jnp.einsum('nbhid,nbhjd->nbhij', qc, kc) * decay

    def step(s, inp):
        q_i, k_i, u_i, w_i, g_i, qk_i = inp
        v_new = u_i - w_i @ s
        o_i = (q_i * jnp.exp(g_i)[..., None]) @ s + qk_i @ v_new
        g_last = g_i[..., -1:]
        s = s * jnp.exp(g_last)[..., None] + jnp.einsum(
            'bhcd,bhce->bhde', k_i * jnp.exp(g_last - g_i)[..., None], v_new)
        return s, o_i

    s_fin, o = lax.scan(step, s0, (qc, kc, u, w, gcum, qk))
    o = jnp.moveaxis(jnp.moveaxis(o, 3, 2), 0, 1)
    return o.reshape(b, l, h, dv), s_fin


def _even_layer(x, pos, w, past):
    g_norm, w_in, w_out, f_bias, q_gain, k_gain, conv_w, a_log, dt_bias, o_gain = w
    b, l, _ = x.shape
    h = _rms(x, g_norm)
    proj = jnp.einsum('bld,de->ble', h, w_in)
    fq, fk, fv, ff, fg, dqkv, dbeta, da, dg = _split(proj, EVEN_SIZES)
    q = _rms(fq.reshape(b, l, FOX_HEADS, HEAD_DIM), q_gain)
    k = _rms(fk.reshape(b, l, FOX_KV_HEADS, HEAD_DIM), k_gain)
    v = fv.reshape(b, l, FOX_KV_HEADS, HEAD_DIM)
    logf = jax.nn.log_sigmoid(ff.astype(F32) + f_bias.astype(F32))
    fox_kv = jnp.stack([k, v], axis=2)
    if past is None:
        kv_all, lf_all = fox_kv, logf
        s0 = jnp.zeros((b, DN_HEADS, DN_DK, DN_DV), F32)
        conv_prev = jnp.zeros((b, CONV_W - 1, DN_CONV_DIM), dqkv.dtype)
    else:
        kv_pool, lf_pool, s0, conv_prev, page_table = past
        kv_all = jnp.concatenate([_gather_pages(kv_pool, page_table), fox_kv], axis=1)
        lf_all = jnp.concatenate([_gather_pages(lf_pool, page_table).astype(F32), logf], axis=1)
    lk = kv_all.shape[1]
    csum = jnp.cumsum(lf_all, axis=1)
    fox_o = _fox_attend(q, kv_all[:, :, 0], kv_all[:, :, 1], csum[:, lk - l:], csum, pos, jnp.arange(lk))
    fox_o = fox_o.reshape(b, l, FOX_WIDTH) * jax.nn.silu(fg)
    xpad = jnp.concatenate([conv_prev.astype(dqkv.dtype), dqkv], axis=1)
    conv = xpad[:, 0:l] * conv_w[0]
    for j in range(1, CONV_W):
        conv = conv + xpad[:, j:j + l] * conv_w[j]
    conv_new = xpad[:, l:]
    conv = jax.nn.silu(conv).astype(F32)
    cq, ck, cv = _split(conv, (DN_HEADS * DN_DK, DN_HEADS * DN_DK, DN_HEADS * DN_DV))
    dq = _l2norm(cq.reshape(b, l, DN_HEADS, DN_DK)) * (DN_DK ** -0.5)
    dk = _l2norm(ck.reshape(b, l, DN_HEADS, DN_DK))
    dv = cv.reshape(b, l, DN_HEADS, DN_DV)
    beta = jax.nn.sigmoid(dbeta.astype(F32))
    gdec = -jnp.exp(a_log.astype(F32)) * jax.nn.softplus(da.astype(F32) + dt_bias.astype(F32))
    dn_o, s_fin = _gated_delta(dq, dk, dv, gdec, beta, s0.astype(F32))
    dn_o = _rms(dn_o.astype(x.dtype), o_gain) * jax.nn.silu(dg.reshape(b, l, DN_HEADS, DN_DV))
    out = jnp.concatenate([fox_o, dn_o.reshape(b, l, DN_WIDTH)], axis=-1)
    y = x + jnp.einsum('ble,ed->bld', out, w_out)
    return y, (fox_kv, logf, s_fin, conv_new)


def _odd_layer(x, pos, w, past, lam_init):
    g_norm, w_in, w_out, cq_gain, ck_gain, dq_gain, dk_gain, lam_p, subln = w
    b, l, _ = x.shape
    h = _rms(x, g_norm)
    proj = jnp.einsum('bld,de->ble', h, w_in)
    cq, ck, cv, iq, ik, iw, cg, dq, dk, dv, dg = _split(proj, ODD_SIZES)
    q = _rope(_rms(cq.reshape(b, l, DSA_HEADS, HEAD_DIM), cq_gain), pos)
    k = _rope(_rms(ck.reshape(b, l, DSA_KV_HEADS, HEAD_DIM), ck_gain), pos)
    v = cv.reshape(b, l, DSA_KV_HEADS, HEAD_DIM)
    qi = _rope(iq.reshape(b, l, IDX_HEADS, IDX_DIM), pos)
    ki = _rope(ik.reshape(b, l, 1, IDX_DIM), pos).reshape(b, l, IDX_DIM)
    wi = iw * (IDX_HEADS ** -0.5)
    dsa_kv = jnp.stack([k, v], axis=2)
    dq_ = _rope(_rms(dq.reshape(b, l, DIFF_HEADS * 2, HEAD_DIM), dq_gain), pos).reshape(b, l, DIFF_HEADS, 2, HEAD_DIM)
    dk_ = _rope(_rms(dk.reshape(b, l, DIFF_KV_HEADS * 2, HEAD_DIM), dk_gain), pos)
    diff_kv = jnp.stack([dk_.reshape(b, l, DIFF_KV_HEADS, 2 * HEAD_DIM),
                         dv.reshape(b, l, DIFF_KV_HEADS, 2 * HEAD_DIM)], axis=2)
    if past is None:
        ki_all = ki
        diff_all = diff_kv
        gather_kv = lambda idx: _gather_seq(dsa_kv, idx)
    else:
        dsa_pool, idx_pool, diff_pool, page_table = past
        ki_all = jnp.concatenate([_gather_pages(idx_pool, page_table), ki], axis=1)
        diff_all = jnp.concatenate([_gather_pages(diff_pool, page_table), diff_kv], axis=1)
        gather_kv = lambda idx: _gather_rows_paged(dsa_pool, page_table, dsa_kv, idx)
    lk = ki_all.shape[1]
    k_pos = jnp.arange(lk)
    dsa_o = _dsa_attend(q, qi, wi, pos, ki_all, k_pos, gather_kv)
    dsa_o = dsa_o.reshape(b, l, DSA_WIDTH) * jax.nn.silu(cg)
    lp = lam_p.astype(F32)
    lam = jnp.exp(jnp.sum(lp[0] * lp[1])) - jnp.exp(jnp.sum(lp[2] * lp[3])) + lam_init
    k_all = diff_all[:, :, 0].reshape(b, lk, DIFF_KV_HEADS, 2, HEAD_DIM)
    v_all = diff_all[:, :, 1]
    diff_o = _diff_attend(dq_, k_all, v_all, lam, pos, k_pos)
    diff_o = (_rms(diff_o, subln) * (1.0 - lam_init)).reshape(b, l, DIFF_WIDTH) * jax.nn.silu(dg)
    out = jnp.concatenate([dsa_o, diff_o], axis=-1)
    y = x + jnp.einsum('ble,ed->bld', out, w_out)
    return y, (dsa_kv, ki, diff_kv)


def _stack(states, j):
    return jnp.stack([s[j] for s in states])


def setup_inputs(seed: int = 0) -> dict:
    key = jax.random.key(seed)
    ks = iter(jax.random.split(key, 40))
    nrm = lambda shape, s=1.0: jax.random.normal(next(ks), shape, F32) * s
    gain = lambda shape: 1.0 + 0.02 * jax.random.normal(next(ks), shape, F32)
    n_pages = PAST_LEN // PAGE_SIZE
    n_used = DEC_BATCH * n_pages
    n_pool = (5 * n_used + 3) // 4
    page_table = jax.random.permutation(next(ks), n_pool)[:n_used].reshape(DEC_BATCH, n_pages).astype(jnp.int32)
    dt = jnp.exp(jax.random.uniform(next(ks), (N_EVEN, DN_HEADS), F32, math.log(1e-3), math.log(1e-1)))
    dn_dt_bias = dt + jnp.log(-jnp.expm1(-dt))
    dn_a_log = jnp.log(jax.random.uniform(next(ks), (N_EVEN, DN_HEADS), F32, 1.0, 16.0))
    return {
        'x_prompt': nrm((BATCH, SEQ, D_MODEL)),
        'x_sample': nrm((DEC_BATCH, DEC_SEQ, D_MODEL)),
        'cache_fox_kv': nrm((N_EVEN, n_pool, PAGE_SIZE, 2, FOX_KV_HEADS, HEAD_DIM)),
        'cache_fox_logf': jax.nn.log_sigmoid(FOX_F_BIAS + nrm((N_EVEN, n_pool, PAGE_SIZE, FOX_HEADS))),
        'state_delta': nrm((N_EVEN, DEC_BATCH, DN_HEADS, DN_DK, DN_DV), 0.1),
        'state_delta_conv': nrm((N_EVEN, DEC_BATCH, CONV_W - 1, DN_CONV_DIM)),
        'cache_dsa_kv': nrm((N_ODD, n_pool, PAGE_SIZE, 2, DSA_KV_HEADS, HEAD_DIM)),
        'cache_dsa_idx_k': nrm((N_ODD, n_pool, PAGE_SIZE, IDX_DIM)),
        'cache_diff_kv': nrm((N_ODD, n_pool, PAGE_SIZE, 2, DIFF_KV_HEADS, 2 * HEAD_DIM)),
        'page_table': page_table,
        'norm_even': gain((N_EVEN, D_MODEL)),
        'w_in_even': nrm((N_EVEN, D_MODEL, EVEN_IN), D_MODEL ** -0.5),
        'w_out_even': nrm((N_EVEN, MIX_WIDTH, D_MODEL), MIX_WIDTH ** -0.5),
        'fox_f_bias': FOX_F_BIAS + nrm((N_EVEN, FOX_HEADS), 0.1),
        'fox_q_norm': gain((N_EVEN, HEAD_DIM)),
        'fox_k_norm': gain((N_EVEN, HEAD_DIM)),
        'dn_conv_w': nrm((N_EVEN, CONV_W, DN_CONV_DIM), CONV_W ** -0.5),
        'dn_a_log': dn_a_log,
        'dn_dt_bias': dn_dt_bias,
        'dn_out_norm': gain((N_EVEN, DN_DV)),
        'norm_odd': gain((N_ODD, D_MODEL)),
        'w_in_odd': nrm((N_ODD, D_MODEL, ODD_IN), D_MODEL ** -0.5),
        'w_out_odd': nrm((N_ODD, MIX_WIDTH, D_MODEL), MIX_WIDTH ** -0.5),
        'dsa_q_norm': gain((N_ODD, HEAD_DIM)),
        'dsa_k_norm': gain((N_ODD, HEAD_DIM)),
        'diff_q_norm': gain((N_ODD, HEAD_DIM)),
        'diff_k_norm': gain((N_ODD, HEAD_DIM)),
        'diff_lambda': nrm((N_ODD, 4, HEAD_DIM), 0.1),
        'diff_subln': gain((N_ODD, 2 * HEAD_DIM)),
    }


def reference(x_prompt, x_sample, cache_fox_kv, cache_fox_logf, state_delta, state_delta_conv,
              cache_dsa_kv, cache_dsa_idx_k, cache_diff_kv, page_table,
              norm_even, w_in_even, w_out_even, fox_f_bias, fox_q_norm, fox_k_norm,
              dn_conv_w, dn_a_log, dn_dt_bias, dn_out_norm,
              norm_odd, w_in_odd, w_out_odd, dsa_q_norm, dsa_k_norm, diff_q_norm, diff_k_norm,
              diff_lambda, diff_subln):
    past_len = page_table.shape[1] * PAGE_SIZE
    pos_p = jnp.arange(x_prompt.shape[1], dtype=jnp.int32)
    pos_s = past_len + jnp.arange(x_sample.shape[1], dtype=jnp.int32)
    xp, xs = x_prompt, x_sample
    even_p, even_s, odd_p, odd_s = [], [], [], []
    for layer in range(DEPTH):
        i = layer // 2
        if layer % 2 == 0:
            w = (norm_even[i], w_in_even[i], w_out_even[i], fox_f_bias[i], fox_q_norm[i], fox_k_norm[i],
                 dn_conv_w[i], dn_a_log[i], dn_dt_bias[i], dn_out_norm[i])
            xp, st = _even_layer(xp, pos_p, w, None)
            even_p.append(st)
            xs, st = _even_layer(xs, pos_s, w, (cache_fox_kv[i], cache_fox_logf[i], state_delta[i],
                                                state_delta_conv[i], page_table))
            even_s.append(st)
        else:
            lam_init = 0.8 - 0.6 * math.exp(-0.3 * layer)
            w = (norm_odd[i], w_in_odd[i], w_out_odd[i], dsa_q_norm[i], dsa_k_norm[i],
                 diff_q_norm[i], diff_k_norm[i], diff_lambda[i], diff_subln[i])
            xp, st = _odd_layer(xp, pos_p, w, None, lam_init)
            odd_p.append(st)
            xs, st = _odd_layer(xs, pos_s, w, (cache_dsa_kv[i], cache_dsa_idx_k[i], cache_diff_kv[i],
                                               page_table), lam_init)
            odd_s.append(st)
    return (xp, xs,
            _stack(even_p, 0), _stack(even_s, 0),
            _stack(even_p, 1), _stack(even_s, 1),
            _stack(even_p, 2), _stack(even_s, 2),
            _stack(even_p, 3), _stack(even_s, 3),
            _stack(odd_p, 0), _stack(odd_s, 0),
            _stack(odd_p, 1), _stack(odd_s, 1),
            _stack(odd_p, 2), _stack(odd_s, 2))
```

```python
import functools
import math

import numpy as np
import jax
import jax.numpy as jnp
from jax import lax
from jax.experimental import pallas as pl
from jax.experimental.pallas import tpu as pltpu

F32 = jnp.float32
BF16 = jnp.bfloat16
I32 = jnp.int32

D_MODEL = 1024
HEAD_DIM = 64
ROT_DIM = HEAD_DIM // 4
ROPE_THETA = 500000.0
EPS = 1e-6
PAGE = 128
DN_HEADS = 4
DN_DK = 128
CONV_W = 4
DN_CHUNK = 64
TOPK_MAX = 256
IDX_HEADS = 4

LANES = 128
HALF_LANES = LANES // 2
VMEM_LIMIT = 56 * 1024 * 1024

NEG_INIT = -1e30
NEG_MASK = -2e30
INT_MIN = -(2 ** 31)

EVEN_OFF = np.cumsum([0, 512, 256, 256, 8, 512, 1536, 4, 4, 512])
ODD_OFF = np.cumsum([0, 512, 256, 256, 256, 64, 4, 512, 512, 256, 256, 512])

HEAD_PERM = np.concatenate([np.arange(64 * n, 64 * n + 64) for n in (0, 2, 1, 3, 4, 6, 5, 7)])


def _cparams(sem):
    return pltpu.CompilerParams(dimension_semantics=sem, vmem_limit_bytes=VMEM_LIMIT)


def _dot(a, b):
    return jnp.dot(a, b, preferred_element_type=F32)


def _dot_nt(a, b):
    return lax.dot_general(a, b, (((1,), (1,)), ((), ())), preferred_element_type=F32)


def _split3(x):
    hi = x.astype(BF16)
    r1 = x - hi.astype(F32)
    mid = r1.astype(BF16)
    lo = (r1 - mid.astype(F32)).astype(BF16)
    return hi, mid, lo


def _split2(x):
    hi = x.astype(BF16)
    lo = (x - hi.astype(F32)).astype(BF16)
    return hi, lo


def _dot_exact_lhs(mask_bf16, x):
    hi, mid, lo = _split3(x)
    return _dot(mask_bf16, hi) + _dot(mask_bf16, mid) + _dot(mask_bf16, lo)


def _dot_exact_rhs(x, mask_bf16):
    hi, mid, lo = _split3(x)
    return _dot(hi, mask_bf16) + _dot(mid, mask_bf16) + _dot(lo, mask_bf16)


def _dot3s(a, b):
    return _dot(a[0], b[0]) + (_dot(a[0], b[1]) + _dot(a[1], b[0]))


def _dot3s_nt(a, b):
    return _dot_nt(a[0], b[0]) + (_dot_nt(a[0], b[1]) + _dot_nt(a[1], b[0]))


def _silu(x):
    return x / (1.0 + jnp.exp(-x))


def _softplus_parts(z):
    return jnp.log1p(jnp.exp(-jnp.abs(z)))


def _lane_lo(shape):
    return lax.broadcasted_iota(I32, shape, len(shape) - 1) < HALF_LANES


def _rms64(x, gain):
    lo = _lane_lo(x.shape)
    x2 = x * x
    s_lo = jnp.sum(jnp.where(lo, x2, 0.0), axis=-1, keepdims=True)
    s_hi = jnp.sum(jnp.where(lo, 0.0, x2), axis=-1, keepdims=True)
    r = jnp.where(lo, lax.rsqrt(s_lo / HEAD_DIM + EPS), lax.rsqrt(s_hi / HEAD_DIM + EPS))
    return x * r * gain


def _rope128(x, tab_ref):
    fwd = pltpu.roll(x, LANES - ROT_DIM // 2, axis=1)
    bwd = pltpu.roll(x, ROT_DIM // 2, axis=1)
    return x * tab_ref[0] + fwd * tab_ref[1] + bwd * tab_ref[2]


def _rope_tables(pos):
    half = ROT_DIM // 2
    inv_freq = 1.0 / (ROPE_THETA ** (jnp.arange(half, dtype=F32) * 2.0 / ROT_DIM))
    ang = pos.astype(F32)[:, None] * inv_freq[None, :]
    cos, sin = jnp.cos(ang), jnp.sin(ang)
    n = pos.shape[0]
    ones = jnp.ones((n, HEAD_DIM - ROT_DIM), F32)
    zeros8 = jnp.zeros((n, half), F32)
    zeros48 = jnp.zeros((n, HEAD_DIM - ROT_DIM), F32)
    c = jnp.concatenate([cos, cos, ones], axis=1)
    sa = jnp.concatenate([-sin, zeros8, zeros48], axis=1)
    sb = jnp.concatenate([zeros8, sin, zeros48], axis=1)
    tab = jnp.stack([c, sa, sb])
    return jnp.concatenate([tab, tab], axis=2)


def _sortable_to_f32(t):
    return pltpu.bitcast(jnp.where(t < 0, t ^ 0x7FFFFFFF, t), F32)


def _kth_largest(count_ge, shape, topk):
    def bit_body(it, t):
        cand = t ^ lax.shift_left(jnp.int32(1), 31 - it)
        return jnp.where(count_ge(_sortable_to_f32(cand)) >= topk, cand, t)

    t = lax.fori_loop(0, 32, bit_body, jnp.full(shape, INT_MIN, I32))
    return jnp.where(t == INT_MIN, -jnp.inf, _sortable_to_f32(t))


def _norm_h(x_ref, g_ref):
    x = x_ref[...]
    ms = jnp.mean(x * x, axis=-1, keepdims=True)
    return (x * lax.rsqrt(ms + EPS) * g_ref[...]).astype(BF16)


def _proj_even_kernel(x_ref, g_ref, w_ref, qg_ref, kg_ref, sp_ref,
                      qb_ref, kv_ref, kvb_ref, small_ref, fg_ref, dqkv_ref, dg_ref):
    h = _norm_h(x_ref, g_ref)

    def mm(a, b):
        return _dot(h, w_ref[:, a:b])

    q = mm(0, 512)
    for c in range(4):
        sl = slice(LANES * c, LANES * (c + 1))
        qb_ref[:, sl] = (_rms64(q[:, sl], qg_ref[...]) * 0.125).astype(BF16)
    kv = mm(512, 1024)
    for c in range(2):
        sl = slice(LANES * c, LANES * (c + 1))
        kn = _rms64(kv[:, sl], kg_ref[...])
        kv_ref[:, sl] = kn
        kvb_ref[:, sl] = kn.astype(BF16)
    kv_ref[:, 256:512] = kv[:, 256:512]
    kvb_ref[:, 256:512] = kv[:, 256:512].astype(BF16)
    fg_ref[...] = mm(1024, 1536)
    dqkv_ref[...] = mm(1536, 3072)
    dg_ref[...] = mm(3072, 3584)
    sm = mm(3584, 3712)
    z = sm + sp_ref[0:1, :]
    l1p = _softplus_parts(z)
    logsig = jnp.minimum(z, 0.0) - l1p
    sig = 1.0 / (1.0 + jnp.exp(-sm))
    gdec = -jnp.exp(sp_ref[1:2, :]) * (jnp.maximum(z, 0.0) + l1p)
    lane = lax.broadcasted_iota(I32, sm.shape, 1)
    small_ref[...] = jnp.where(lane < 8, logsig, jnp.where(lane < 12, sig, jnp.where(lane < 16, gdec, 0.0)))


def _proj_odd_kernel(x_ref, g_ref, w_ref, tab_ref, cqg_ref, ckg_ref, dqg_ref, dkg_ref,
                     qb_ref, kv_ref, kvb_ref, qib_ref, kib_ref, idx_ref, wi_ref, cg_ref,
                     dqb_ref, dkv_ref, dkvb_ref, dg_ref):
    h = _norm_h(x_ref, g_ref)

    def mm(a, b):
        return _dot(h, w_ref[:, a:b])

    q = mm(0, 512)
    for c in range(4):
        sl = slice(LANES * c, LANES * (c + 1))
        qb_ref[:, sl] = (_rope128(_rms64(q[:, sl], cqg_ref[...]), tab_ref) * 0.125).astype(BF16)
    kv = mm(512, 1024)
    for c in range(2):
        sl = slice(LANES * c, LANES * (c + 1))
        kn = _rope128(_rms64(kv[:, sl], ckg_ref[...]), tab_ref)
        kv_ref[:, sl] = kn
        kvb_ref[:, sl] = kn.astype(BF16)
    kv_ref[:, 256:512] = kv[:, 256:512]
    kvb_ref[:, 256:512] = kv[:, 256:512].astype(BF16)
    qi = mm(1024, 1280)
    for c in range(2):
        sl = slice(LANES * c, LANES * (c + 1))
        qib_ref[:, sl] = (_rope128(qi[:, sl], tab_ref) * 0.125).astype(BF16)
    ki = _rope128(mm(1280, 1408), tab_ref)
    kib_ref[...] = ki.astype(BF16)
    idx_ref[...] = ki[:, :HEAD_DIM]
    wi_ref[...] = mm(1408, 1536) * 0.5
    cg_ref[...] = mm(1536, 2048)
    dq = mm(2048, 2560)
    for c in range(4):
        sl = slice(LANES * c, LANES * (c + 1))
        dqb_ref[:, sl] = (_rope128(_rms64(dq[:, sl], dqg_ref[...]), tab_ref) * 0.125).astype(BF16)
    dkv = mm(2560, 3072)
    for c in range(2):
        sl = slice(LANES * c, LANES * (c + 1))
        kn = _rope128(_rms64(dkv[:, sl], dkg_ref[...]), tab_ref)
        dkv_ref[:, sl] = kn
        dkvb_ref[:, sl] = kn.astype(BF16)
    dkv_ref[:, 256:512] = dkv[:, 256:512]
    dkvb_ref[:, 256:512] = dkv[:, 256:512].astype(BF16)
    dg_ref[...] = mm(3072, 3584)


def _row_tile(m):
    return 256 if m % 256 == 0 else m


def _tile2(g):
    return jnp.concatenate([g, g]).reshape(1, LANES).astype(F32)


def _proj_even(x2d, g_norm, w_bf, q_gain, k_gain, f_bias, a_log, dt_bias):
    m = x2d.shape[0]
    tm = _row_tile(m)
    sp = jnp.zeros((8, LANES), F32)
    sp = sp.at[0, 0:8].set(f_bias).at[0, 12:16].set(dt_bias).at[1, 12:16].set(a_log)
    row = lambda w: pl.BlockSpec((tm, w), lambda i: (i, 0))
    full = lambda a: pl.BlockSpec(a.shape, lambda i: (0,) * a.ndim)
    g2 = g_norm.reshape(1, D_MODEL)
    qg, kg = _tile2(q_gain), _tile2(k_gain)
    outs = [(512, BF16), (512, F32), (512, BF16), (LANES, F32), (512, F32), (1536, F32), (512, F32)]
    return pl.pallas_call(
        _proj_even_kernel,
        grid=(m // tm,),
        in_specs=[row(D_MODEL), full(g2), full(w_bf), full(qg), full(kg), full(sp)],
        out_specs=[row(w) for w, _ in outs],
        out_shape=[jax.ShapeDtypeStruct((m, w), dt) for w, dt in outs],
        compiler_params=_cparams(("parallel",)),
        name="proj_even",
    )(x2d, g2, w_bf, qg, kg, sp)


def _proj_odd(x2d, g_norm, w_bf, tab, n_tab_blocks, cq_gain, ck_gain, dq_gain, dk_gain):
    m = x2d.shape[0]
    tm = _row_tile(m)
    row = lambda w: pl.BlockSpec((tm, w), lambda i: (i, 0))
    full = lambda a: pl.BlockSpec(a.shape, lambda i: (0,) * a.ndim)
    g2 = g_norm.reshape(1, D_MODEL)
    gains = [_tile2(g) for g in (cq_gain, ck_gain, dq_gain, dk_gain)]
    tab_spec = pl.BlockSpec((3, tm, LANES), lambda i: (0, i % n_tab_blocks, 0))
    outs = [(512, BF16), (512, F32), (512, BF16), (256, BF16), (LANES, BF16), (HEAD_DIM, F32), (LANES, F32),
            (512, F32), (512, BF16), (512, F32), (512, BF16), (512, F32)]
    return pl.pallas_call(
        _proj_odd_kernel,
        grid=(m // tm,),
        in_specs=[row(D_MODEL), full(g2), full(w_bf), tab_spec] + [full(g) for g in gains],
        out_specs=[row(w) for w, _ in outs],
        out_shape=[jax.ShapeDtypeStruct((m, w), dt) for w, dt in outs],
        compiler_params=_cparams(("parallel",)),
        name="proj_odd",
    )(x2d, g2, w_bf, tab, *gains)


def _out_proj_kernel(x_ref, a_ref, b_ref, wa_ref, wb_ref, y_ref):
    y_ref[...] = x_ref[...] + _dot(a_ref[...], wa_ref[...]) + _dot(b_ref[...], wb_ref[...])


def _out_proj(x2d, oa, ob, wa, wb):
    m = x2d.shape[0]
    tm = _row_tile(m)
    row = lambda w: pl.BlockSpec((tm, w), lambda i: (i, 0))
    full = lambda a: pl.BlockSpec(a.shape, lambda i: (0,) * a.ndim)
    return pl.pallas_call(
        _out_proj_kernel,
        grid=(m // tm,),
        in_specs=[row(D_MODEL), row(512), row(512), full(wa), full(wb)],
        out_specs=row(D_MODEL),
        out_shape=jax.ShapeDtypeStruct((m, D_MODEL), F32),
        compiler_params=_cparams(("parallel",)),
        name="out_proj",
    )(x2d, oa, ob, wa, wb)


def _cumsum_kernel(s_ref, c_ref, ct_ref, carry_ref):
    @pl.when(pl.program_id(1) == 0)
    def _():
        carry_ref[...] = jnp.zeros_like(carry_ref)

    x = s_ref[...]
    tc = x.shape[0]
    r = lax.broadcasted_iota(I32, (tc, tc), 0)
    c = lax.broadcasted_iota(I32, (tc, tc), 1)
    tri = jnp.where(r >= c, 1.0, 0.0).astype(BF16)
    cs = _dot_exact_lhs(tri, x) + carry_ref[0:1, :]
    carry_ref[0:1, :] = cs[tc - 1:tc, :]
    c_ref[...] = cs
    ct_ref[0] = cs.T[0:8, :]


def _cumsum_logf(small, batch, seq):
    tc = 256 if seq % 256 == 0 else seq
    nb = seq // tc
    return pl.pallas_call(
        _cumsum_kernel,
        grid=(batch, nb),
        in_specs=[pl.BlockSpec((tc, LANES), lambda b, j: (b * nb + j, 0))],
        out_specs=[pl.BlockSpec((tc, LANES), lambda b, j: (b * nb + j, 0)),
                   pl.BlockSpec((1, 8, tc), lambda b, j: (b, 0, j))],
        out_shape=[jax.ShapeDtypeStruct((batch * seq, LANES), F32),
                   jax.ShapeDtypeStruct((batch, 8, seq), F32)],
        scratch_shapes=[pltpu.VMEM((8, LANES), F32)],
        compiler_params=_cparams(("parallel", "arbitrary")),
        name="cumsum_logf",
    )(small)


def _lane_chunks(x):
    return [x[:, LANES * g:LANES * (g + 1)] for g in range(x.shape[1] // LANES)]


def _flash_update(j, chunks, vch, m_scr, l_scr, acc_scr):
    mx = chunks[0]
    for ch in chunks[1:]:
        mx = jnp.maximum(mx, ch)
    m_prev = m_scr[j]
    m_new = jnp.maximum(m_prev, jnp.max(mx, axis=-1, keepdims=True))
    alpha = jnp.exp(m_prev - m_new)
    ps = [jnp.exp(ch - m_new) for ch in chunks]
    tot = ps[0]
    for p in ps[1:]:
        tot = tot + p
    l_scr[j] = alpha * l_scr[j] + jnp.sum(tot, axis=-1, keepdims=True)
    p_bf = jnp.concatenate([p.astype(BF16) for p in ps], axis=1)
    acc_scr[j] = alpha * acc_scr[j] + _dot(p_bf, vch)
    m_scr[j] = m_new


def _build_qm(q_ref, qm_scr, tq):
    lo = _lane_lo((tq, LANES))
    zero = jnp.zeros((tq, LANES), BF16)
    for j in range(2):
        a = q_ref[:, 256 * j:256 * j + LANES]
        b = q_ref[:, 256 * j + LANES:256 * j + 2 * LANES]
        qm_scr[j, 0:tq] = jnp.where(lo, a, zero)
        qm_scr[j, tq:2 * tq] = jnp.where(lo, b, zero)
        qm_scr[j, 2 * tq:3 * tq] = jnp.where(lo, zero, a)
        qm_scr[j, 3 * tq:4 * tq] = jnp.where(lo, zero, b)


def _attn_kernel(*refs, mode, tq, tk, lam_init, topk):
    if mode == "fox":
        q_ref, kv_ref, c_ref, ct_ref, gate_ref, o_ref, qm_scr, m_scr, l_scr, acc_scr, cb_scr = refs
    elif mode == "diff":
        q_ref, kv_ref, gate_ref, lamp_ref, subln_ref, o_ref, qm_scr, m_scr, l_scr, acc_scr = refs
    else:
        (q_ref, kv_ref, qi_ref, ki_ref, wi_ref, ust_ref, gate_ref, o_ref,
         qm_scr, m_scr, l_scr, acc_scr, qim_scr, isc_scr) = refs

    i = pl.program_id(1)
    ngrp = tk // LANES
    _build_qm(q_ref, qm_scr, tq)
    m_scr[...] = jnp.full(m_scr.shape, NEG_INIT, F32)
    l_scr[...] = jnp.zeros_like(l_scr)
    acc_scr[...] = jnp.zeros_like(acc_scr)
    nfull = (i * tq) // tk
    qpos1 = i * tq + lax.broadcasted_iota(I32, (tq, LANES), 0)
    kiota = lax.broadcasted_iota(I32, (tq, LANES), 1)

    if mode == "fox":
        for j in range(2):
            for r in range(4):
                n = 4 * j + r
                cb_scr[j, r * tq:(r + 1) * tq] = jnp.broadcast_to(c_ref[:, n:n + 1], (tq, LANES))

    if mode == "dsa":
        lo = _lane_lo((tq, LANES))
        zero = jnp.zeros((tq, LANES), BF16)
        for h in range(IDX_HEADS):
            ch = qi_ref[:, LANES * (h // 2):LANES * (h // 2 + 1)]
            qim_scr[h * tq:(h + 1) * tq] = jnp.where(lo, ch, zero) if h % 2 == 0 else jnp.where(lo, zero, ch)
        wcols = [jnp.broadcast_to(wi_ref[:, h:h + 1], (tq, LANES)) for h in range(IDX_HEADS)]

        def idx_chunk(kc, diagonal):
            ks = pl.multiple_of(kc * tk, tk)
            sc = jnp.maximum(_dot_nt(qim_scr[...], ki_ref[pl.ds(ks, tk), :]), 0.0)
            for g in range(ngrp):
                gs = slice(LANES * g, LANES * (g + 1))
                isc = sc[0:tq, gs] * wcols[0]
                for h in range(1, IDX_HEADS):
                    isc = isc + sc[h * tq:(h + 1) * tq, gs] * wcols[h]
                if diagonal:
                    isc = jnp.where(ks + LANES * g + kiota <= qpos1, isc, -jnp.inf)
                isc_scr[:, pl.ds(pl.multiple_of(ks + LANES * g, LANES), LANES)] = isc

        def idx_body(kc, carry):
            idx_chunk(kc, False)
            return carry

        lax.fori_loop(0, nfull, idx_body, 0)
        idx_chunk(nfull, True)

        def count(pred):
            def body(kc, acc):
                ks = pl.multiple_of(kc * tk, tk)
                for g in range(ngrp):
                    x = isc_scr[:, pl.ds(pl.multiple_of(ks + LANES * g, LANES), LANES)]
                    acc = acc + jnp.where(pred(x), 1.0, 0.0)
                return acc
            acc = lax.fori_loop(0, nfull + 1, body, jnp.zeros((tq, LANES), F32))
            return jnp.sum(acc, axis=-1, keepdims=True)

        thr = _kth_largest(lambda cand: count(lambda x: x >= cand), (tq, LANES), topk)
        need = topk - count(lambda x: x > thr)

    def chunk(kc, rank, diagonal):
        ks = pl.multiple_of(kc * tk, tk)
        if mode == "dsa":
            sel, eqs = [], []
            for g in range(ngrp):
                x = isc_scr[:, pl.ds(pl.multiple_of(ks + LANES * g, LANES), LANES)]
                eq = (x == thr) & (x > -jnp.inf)
                eqs.append(jnp.where(eq, 1.0, 0.0))
                sel.append((x > thr, eq))
            eqf = jnp.concatenate(eqs, axis=1)
            prefix = _dot(eqf.astype(BF16), ust_ref[...]) + rank
            sel = [gt | (eq & (prefix[:, LANES * g:LANES * (g + 1)] < need)) for g, (gt, eq) in enumerate(sel)]
            rank = rank + jnp.sum(eqf, axis=-1, keepdims=True)
        for j in range(2):
            kch = kv_ref[pl.ds(ks, tk), LANES * j:LANES * (j + 1)]
            vch = kv_ref[pl.ds(ks, tk), 256 + LANES * j:256 + LANES * (j + 1)]
            s = _dot_nt(qm_scr[j], kch)
            blocks = []
            for r in range(4):
                chunks = _lane_chunks(s[r * tq:(r + 1) * tq])
                if mode == "fox":
                    n = 4 * j + r
                    cb = cb_scr[j, r * tq:(r + 1) * tq]
                    chunks = [ch + (cb - ct_ref[0, n:n + 1, pl.ds(pl.multiple_of(ks + LANES * g, LANES), LANES)])
                              for g, ch in enumerate(chunks)]
                if mode == "dsa":
                    chunks = [jnp.where(sel[g], ch, NEG_MASK) for g, ch in enumerate(chunks)]
                elif diagonal:
                    chunks = [jnp.where(ks + LANES * g + kiota <= qpos1, ch, NEG_MASK) for g, ch in enumerate(chunks)]
                blocks.append(chunks)
            chunks = [jnp.concatenate([blocks[r][g] for r in range(4)], axis=0) for g in range(ngrp)]
            _flash_update(j, chunks, vch, m_scr, l_scr, acc_scr)
        return rank

    rank = lax.fori_loop(0, nfull, lambda kc, rank: chunk(kc, rank, False), jnp.zeros((tq, 1), F32))
    chunk(nfull, rank, True)

    lo = _lane_lo((tq, LANES))
    if mode == "diff":
        lp = lamp_ref[...]
        lam = (jnp.exp(jnp.sum(lp[0:1] * lp[1:2], axis=-1, keepdims=True))
               - jnp.exp(jnp.sum(lp[2:3] * lp[3:4], axis=-1, keepdims=True)) + lam_init)
    for j in range(2):
        o = acc_scr[j] / l_scr[j]
        if mode == "diff":
            heads = (o[0:tq] - lam * o[2 * tq:3 * tq], o[tq:2 * tq] - lam * o[3 * tq:4 * tq])
            outs = []
            for ho in heads:
                ms = jnp.mean(ho * ho, axis=-1, keepdims=True)
                outs.append(ho * lax.rsqrt(ms + EPS) * subln_ref[...] * (1.0 - lam_init))
        else:
            outs = (jnp.where(lo, o[0:tq], o[2 * tq:3 * tq]), jnp.where(lo, o[tq:2 * tq], o[3 * tq:4 * tq]))
        for t, val in enumerate(outs):
            sl = slice(256 * j + LANES * t, 256 * j + LANES * (t + 1))
            o_ref[:, sl] = (val * _silu(gate_ref[:, sl])).astype(BF16)


def _attn_tiles(seq):
    tq = 128 if seq % 128 == 0 else seq
    tk = 512 if seq % 512 == 0 else seq
    return tq, tk


def _prompt_attention(mode, batch, seq, q, kv, gate, extra, lam_init=0.0):
    tq, tk = _attn_tiles(seq)
    nq = seq // tq
    topk = min(TOPK_MAX, seq // 4)
    qrow = lambda w: pl.BlockSpec((tq, w), lambda b, i: (b * nq + i, 0))
    seqblk = lambda w: pl.BlockSpec((seq, w), lambda b, i: (b, 0))
    full = lambda a: pl.BlockSpec(a.shape, lambda b, i: (0,) * a.ndim)
    stat = pltpu.VMEM((2, 4 * tq, LANES), F32)
    scratch = [pltpu.VMEM((2, 4 * tq, LANES), BF16), stat, stat, stat]
    if mode == "fox":
        c, ct = extra
        args = (q, kv, c, ct, gate)
        in_specs = [qrow(512), seqblk(512), qrow(LANES), pl.BlockSpec((1, 8, seq), lambda b, i: (b, 0, 0)), qrow(512)]
        scratch += [stat]
    elif mode == "diff":
        lamp, subln = extra
        args = (q, kv, gate, lamp, subln)
        in_specs = [qrow(512), seqblk(512), qrow(512), full(lamp), full(subln)]
    else:
        qi, ki, wi = extra
        r = lax.broadcasted_iota(I32, (tk, tk), 0)
        c_ = lax.broadcasted_iota(I32, (tk, tk), 1)
        ust = jnp.where(r < c_, 1.0, 0.0).astype(BF16)
        args = (q, kv, qi, ki, wi, ust, gate)
        in_specs = [qrow(512), seqblk(512), qrow(256), seqblk(LANES), qrow(LANES), full(ust), qrow(512)]
        scratch += [pltpu.VMEM((4 * tq, LANES), BF16), pltpu.VMEM((tq, seq), F32)]
    return pl.pallas_call(
        functools.partial(_attn_kernel, mode=mode, tq=tq, tk=tk, lam_init=lam_init, topk=topk),
        grid=(batch, nq),
        in_specs=in_specs,
        out_specs=qrow(512),
        out_shape=jax.ShapeDtypeStruct((batch * seq, 512), BF16),
        scratch_shapes=scratch,
        compiler_params=_cparams(("parallel", "arbitrary")),
        name="attn_" + mode,
    )(*args)


def _l2n(x):
    return x * lax.rsqrt(jnp.sum(x * x, axis=-1, keepdims=True) + EPS)


def _stack_heads(f):
    return jnp.concatenate([f(h) for h in range(DN_HEADS)], axis=0)


def _block_cols(x):
    c = DN_CHUNK
    zero = jnp.zeros((c, LANES), x.dtype)
    rows = []
    for h in range(DN_HEADS):
        rows.append(jnp.concatenate([x[c * h:c * (h + 1)] if g == h else zero for g in range(DN_HEADS)], axis=1))
    return jnp.concatenate(rows, axis=0)


def _delta_prep(conv, small):
    c = DN_CHUNK
    n = DN_HEADS * c
    q = _stack_heads(lambda h: _l2n(conv[:, LANES * h:LANES * (h + 1)]) * (DN_DK ** -0.5))
    k = _stack_heads(lambda h: _l2n(conv[:, 512 + LANES * h:512 + LANES * (h + 1)]))
    v = _stack_heads(lambda h: conv[:, 1024 + LANES * h:1024 + LANES * (h + 1)])
    beta = _stack_heads(lambda h: small[:, 8 + h:9 + h])
    r64 = lax.broadcasted_iota(I32, (c, c), 0)
    c64 = lax.broadcasted_iota(I32, (c, c), 1)
    gc = _dot_exact_lhs(jnp.where(r64 >= c64, 1.0, 0.0).astype(BF16), small)
    gct = gc.T
    gcol = _stack_heads(lambda h: gc[:, 12 + h:13 + h])
    grow = jnp.concatenate([gct[12 + h:13 + h, :] for h in range(DN_HEADS)], axis=1)
    glast = _stack_heads(lambda h: jnp.broadcast_to(gc[c - 1:c, 12 + h:13 + h], (c, 1)))
    r = lax.broadcasted_iota(I32, (n, n), 0)
    cc = lax.broadcasted_iota(I32, (n, n), 1)
    same = (r // c) == (cc // c)
    decay = jnp.exp(jnp.where(same & (r >= cc), gcol - grow, NEG_INIT))
    kb = k * beta
    ks, qs = _split2(k), _split2(q)
    a_mat = jnp.where(same & (r > cc), _dot3s_nt(_split2(kb), ks) * decay, 0.0)
    eye = jnp.where(r == cc, 1.0, 0.0)
    x = eye
    size = 1
    while size < c:
        off = ((r // (2 * size)) == (cc // (2 * size))) & ((r // size) % 2 == 1) & ((cc // size) % 2 == 0)
        xb = x.astype(BF16)
        x = x - _dot(_dot(xb, jnp.where(off, a_mat, 0.0).astype(BF16)).astype(BF16), xb)
        size *= 2
    xs = _split2(x)
    x = x + _dot3s(xs, _split2(eye - _dot3s(_split2(eye + a_mat), xs)))
    uw = _dot3s(_split2(x), _split2(jnp.concatenate([v * beta, kb * jnp.exp(gcol)], axis=1)))
    u, w = uw[:, 0:LANES], uw[:, LANES:2 * LANES]
    qk = _dot3s_nt(qs, ks) * decay
    qg = q * jnp.exp(gcol)
    kdec = k * jnp.exp(glast - gcol)
    eg = jnp.concatenate([jnp.broadcast_to(jnp.exp(gc[c - 1:c, 12 + h:13 + h]), (DN_DK, 1)) for h in range(DN_HEADS)],
                         axis=0)
    return u, _split2(_block_cols(w)), _split2(qk), _split2(_block_cols(qg)), _split2(_block_cols(kdec).T), eg


def _delta_scan(st, prep):
    u, wb, qk, qgb, kdt, eg = prep
    sts = _split2(st)
    v_new = u - _dot3s(wb, sts)
    vs = _split2(v_new)
    o = _dot3s(qgb, sts) + _dot3s(qk, vs)
    return st * eg + _dot3s(kdt, vs), o


def _delta_prompt_kernel(x_ref, small_ref, dg_ref, cw_ref, og_ref, o_ref, s_ref, cn_ref, xpad_scr, st_scr):
    i = pl.program_id(1)
    n = pl.num_programs(1)
    c = DN_CHUNK
    rows = x_ref.shape[0]

    @pl.when(i == 0)
    def _():
        xpad_scr[0:8, :] = jnp.zeros((8, xpad_scr.shape[1]), F32)
        st_scr[...] = jnp.zeros_like(st_scr)

    xpad_scr[8:8 + rows, :] = x_ref[...]
    conv = xpad_scr[5:5 + rows, :] * cw_ref[0:1, :]
    for j in range(1, CONV_W):
        conv = conv + xpad_scr[5 + j:5 + j + rows, :] * cw_ref[j:j + 1, :]
    conv = _silu(conv)
    tail = xpad_scr[rows:rows + 8, :]
    xpad_scr[0:8, :] = tail

    @pl.when(i == n - 1)
    def _():
        cn_ref[0] = tail[5:8, :]

    small = small_ref[...]
    preps = [_delta_prep(conv[c * t:c * (t + 1)], small[c * t:c * (t + 1)]) for t in range(rows // c)]
    st = st_scr[...]
    for t, prep in enumerate(preps):
        st, o = _delta_scan(st, prep)
        for h in range(DN_HEADS):
            hs = slice(LANES * h, LANES * (h + 1))
            oh = o[c * h:c * (h + 1)]
            ms = jnp.mean(oh * oh, axis=-1, keepdims=True)
            o_ref[c * t:c * (t + 1), hs] = (oh * lax.rsqrt(ms + EPS) * og_ref[...]
                                            * _silu(dg_ref[c * t:c * (t + 1), hs])).astype(BF16)
    st_scr[...] = st

    @pl.when(i == n - 1)
    def _():
        for h in range(DN_HEADS):
            s_ref[0, h] = st[DN_DK * h:DN_DK * (h + 1)]


def _delta_prompt(batch, seq, dqkv, small, dg, conv_w, o_gain):
    rows = 2 * DN_CHUNK if seq % (2 * DN_CHUNK) == 0 else DN_CHUNK
    n = seq // rows
    cdim = dqkv.shape[1]
    row = lambda w: pl.BlockSpec((rows, w), lambda b, i: (b * n + i, 0))
    full = lambda a: pl.BlockSpec(a.shape, lambda b, i: (0,) * a.ndim)
    og = o_gain.reshape(1, LANES)
    return pl.pallas_call(
        _delta_prompt_kernel,
        grid=(batch, n),
        in_specs=[row(cdim), row(LANES), row(512), full(conv_w), full(og)],
        out_specs=[row(512),
                   pl.BlockSpec((1, DN_HEADS, DN_DK, LANES), lambda b, i: (b, 0, 0, 0)),
                   pl.BlockSpec((1, CONV_W - 1, cdim), lambda b, i: (b, 0, 0))],
        out_shape=[jax.ShapeDtypeStruct((batch * seq, 512), BF16),
                   jax.ShapeDtypeStruct((batch, DN_HEADS, DN_DK, LANES), F32),
                   jax.ShapeDtypeStruct((batch, CONV_W - 1, cdim), F32)],
        scratch_shapes=[pltpu.VMEM((rows + 8, cdim), F32), pltpu.VMEM((DN_HEADS * DN_DK, LANES), F32)],
        compiler_params=_cparams(("parallel", "arbitrary")),
        name="delta_prompt",
    )(dqkv, small, dg, conv_w, og)


def _delta_decode_kernel(x_ref, prev_ref, s0_ref, small_ref, dg_ref, cw_ref, og_ref, o_ref, s_ref, cn_ref):
    xn = x_ref[0]
    prev = prev_ref[0, 0]
    conv = prev[0:1] * cw_ref[0:1, :] + prev[1:2] * cw_ref[1:2, :] + prev[2:3] * cw_ref[2:3, :] + xn * cw_ref[3:4, :]
    conv = _silu(conv)
    cn_ref[0, 0:2, :] = prev[1:3]
    cn_ref[0, 2:3, :] = xn
    small = small_ref[0]
    for h in range(DN_HEADS):
        hs = slice(LANES * h, LANES * (h + 1))
        q = _l2n(conv[:, LANES * h:LANES * (h + 1)]) * (DN_DK ** -0.5)
        k = _l2n(conv[:, 512 + LANES * h:512 + LANES * (h + 1)])
        v = conv[:, 1024 + LANES * h:1024 + LANES * (h + 1)]
        beta = small[:, 8 + h:9 + h]
        eg = jnp.exp(small[:, 12 + h:13 + h])
        st = s0_ref[0, 0, h]
        col = lambda row: jnp.broadcast_to(row, (DN_DK, LANES)).T
        kcol = col(k)
        v_new = v * beta - jnp.sum(col(k * beta * eg) * st, axis=0, keepdims=True)
        qk = jnp.sum(q * k, axis=-1, keepdims=True)
        o = jnp.sum(col(q * eg) * st, axis=0, keepdims=True) + qk * v_new
        s_ref[0, h] = st * eg + kcol * v_new
        ms = jnp.mean(o * o, axis=-1, keepdims=True)
        o_ref[0, :, hs] = (o * lax.rsqrt(ms + EPS) * og_ref[...] * _silu(dg_ref[0, :, hs])).astype(BF16)


def _delta_decode(layer, dqkv, conv_prev_all, s0_all, small, dg, conv_w, o_gain):
    b, cdim = dqkv.shape
    og = o_gain.reshape(1, LANES)
    r3 = lambda a: a.reshape(b, 1, a.shape[-1])
    blk = lambda shape: pl.BlockSpec((1,) + shape, lambda i: (i,) + (0,) * len(shape))
    lay = lambda shape: pl.BlockSpec((1, 1) + shape, lambda i: (layer, i) + (0,) * len(shape))
    full = lambda a: pl.BlockSpec(a.shape, lambda i: (0,) * a.ndim)
    return pl.pallas_call(
        _delta_decode_kernel,
        grid=(b,),
        in_specs=[blk((1, cdim)), lay((CONV_W - 1, cdim)), lay((DN_HEADS, DN_DK, LANES)), blk((1, LANES)),
                  blk((1, 512)), full(conv_w), full(og)],
        out_specs=[blk((1, 512)), blk((DN_HEADS, DN_DK, LANES)), blk((CONV_W - 1, cdim))],
        out_shape=[jax.ShapeDtypeStruct((b, 1, 512), BF16),
                   jax.ShapeDtypeStruct((b, DN_HEADS, DN_DK, LANES), F32),
                   jax.ShapeDtypeStruct((b, CONV_W - 1, cdim), F32)],
        compiler_params=_cparams(("parallel",)),
        name="delta_decode",
    )(r3(dqkv), conv_prev_all, s0_all, r3(small), r3(dg), conv_w, og)


PAGES_PER_STEP = 8


def _head_rows(q_ref, mode):
    qrow = q_ref[0].astype(F32)
    r = lax.broadcasted_iota(I32, (8, LANES), 0)
    lo = _lane_lo((8, LANES))
    chunks = []
    for j in range(2):
        a = qrow[:, 256 * j:256 * j + LANES]
        b = qrow[:, 256 * j + LANES:256 * j + 2 * LANES]
        rr = r - 4 * j
        if mode == "diff":
            val = jnp.where(rr < 2, a, b)
            keep = (rr >= 0) & (rr < 4) & (lo == ((rr % 2) == 0))
        else:
            val = jnp.where((rr % 2) == 0, a, b)
            keep = (rr >= 0) & (rr < 4) & (lo == (rr < 2))
        chunks.append(jnp.where(keep, val, 0.0))
    return jnp.concatenate(chunks, axis=1)


def _decode_finish(mode, acc, l, gate_ref, o_ref, lam=None, subln_ref=None, lam_init=0.0):
    o = acc / jnp.concatenate([l, l], axis=1)
    lo = _lane_lo((1, LANES))
    for j in range(2):
        blk = o[:, LANES * j:LANES * (j + 1)]
        r0, r1, r2, r3 = (blk[4 * j + t:4 * j + t + 1] for t in range(4))
        if mode == "diff":
            outs = []
            for ho in (r0 - lam * r1, r2 - lam * r3):
                ms = jnp.mean(ho * ho, axis=-1, keepdims=True)
                outs.append(ho * lax.rsqrt(ms + EPS) * subln_ref[...] * (1.0 - lam_init))
        else:
            outs = (jnp.where(lo, r0, r2), jnp.where(lo, r1, r3))
        for t, val in enumerate(outs):
            sl = slice(256 * j + LANES * t, 256 * j + LANES * (t + 1))
            o_ref[0, :, sl] = (val * _silu(gate_ref[0, :, sl])).astype(BF16)


def _lanes_to_col(row):
    r = lax.broadcasted_iota(I32, (8, LANES), 0)
    c = lax.broadcasted_iota(I32, (8, LANES), 1)
    return jnp.sum(jnp.where(r == c, row, 0.0), axis=-1, keepdims=True)


def _decode_attn_kernel(*refs, mode, lam_init, pps):
    q_ref, newkv_ref, gate_ref = refs[1:4]
    pos = 4
    if mode == "fox":
        newsmall_ref = refs[pos]; pos += 1
    elif mode == "diff":
        lamp_ref, subln_ref = refs[pos:pos + 2]; pos += 2
    else:
        sel_ref, selnew_ref = refs[pos:pos + 2]; pos += 2
    kv_refs = refs[pos:pos + pps]; pos += pps
    if mode == "fox":
        lf_refs = refs[pos:pos + pps]; pos += pps
    o_ref = refs[pos]; pos += 1
    qh_scr, m_scr, l_scr, acc_scr, carry_scr = refs[pos:pos + 5]

    s_idx = pl.program_id(1)
    nsteps = pl.num_programs(1)

    @pl.when(s_idx == 0)
    def _():
        qh = _head_rows(q_ref, mode)
        qh_scr[...] = qh
        newkv = newkv_ref[0]
        s_new = jnp.sum(qh * newkv[:, 0:256], axis=-1, keepdims=True)
        if mode == "dsa":
            keep = selnew_ref[0, :, 0:1] > 0.5
            m_scr[...] = jnp.broadcast_to(jnp.where(keep, s_new, NEG_INIT), (8, LANES))
            l_scr[...] = jnp.broadcast_to(jnp.where(keep, 1.0, 0.0), (8, LANES))
            acc_scr[...] = jnp.where(keep, 1.0, 0.0) * jnp.broadcast_to(newkv[:, 256:512], (8, 256))
        else:
            m_scr[...] = jnp.broadcast_to(s_new, (8, LANES))
            l_scr[...] = jnp.ones((8, LANES), F32)
            acc_scr[...] = jnp.broadcast_to(newkv[:, 256:512], (8, 256))
        if mode == "fox":
            carry_scr[...] = jnp.broadcast_to(_lanes_to_col(newsmall_ref[0]), (8, LANES))

    qh = qh_scr[...].astype(BF16)
    if mode == "diff":
        rows_of = lambda t, first: kv_refs[t][0, 0, pl.ds(first, PAGE, stride=4), :].astype(BF16)
        ss = [_dot_nt(qh[:, 0:LANES], rows_of(t, 0)) + _dot_nt(qh[:, LANES:2 * LANES], rows_of(t, 1))
              for t in range(pps)]
        pv_of = lambda t, p: jnp.concatenate([_dot(p, rows_of(t, 2)), _dot(p, rows_of(t, 3))], axis=1)
    else:
        ss = [_dot(qh, kv_refs[t][0, 0, 0].astype(BF16)) for t in range(pps)]
        pv_of = lambda t, p: _dot_nt(p, kv_refs[t][0, 0, 1].astype(BF16))
    if mode == "fox":
        r = lax.broadcasted_iota(I32, (PAGE, PAGE), 0)
        c = lax.broadcasted_iota(I32, (PAGE, PAGE), 1)
        later = jnp.where(r > c, 1.0, 0.0).astype(BF16)
        carry = carry_scr[...]
        for t in range(pps):
            lf = lf_refs[t][0, 0]
            ss[t] = ss[t] + (_dot_exact_rhs(lf, later) + carry)
            carry = carry + jnp.sum(lf, axis=-1, keepdims=True)
        carry_scr[...] = carry
    if mode == "dsa":
        ss = [jnp.where(sel_ref[0, pl.ds(s_idx * pps + t, 1), :] > 0.5, ss[t], NEG_MASK) for t in range(pps)]
    mx = ss[0]
    for s in ss[1:]:
        mx = jnp.maximum(mx, s)
    m_prev = m_scr[...]
    m_new = jnp.maximum(m_prev, jnp.max(mx, axis=-1, keepdims=True))
    alpha = jnp.exp(m_prev - m_new)
    ps = [jnp.exp(s - m_new) for s in ss]
    tot = ps[0]
    for p in ps[1:]:
        tot = tot + p
    l_scr[...] = alpha * l_scr[...] + jnp.sum(tot, axis=-1, keepdims=True)
    pv = pv_of(0, ps[0].astype(BF16))
    for t in range(1, pps):
        pv = pv + pv_of(t, ps[t].astype(BF16))
    acc_scr[...] = jnp.concatenate([alpha, alpha], axis=1) * acc_scr[...] + pv
    m_scr[...] = m_new

    @pl.when(s_idx == nsteps - 1)
    def _():
        if mode == "diff":
            lp = lamp_ref[...]
            lam = (jnp.exp(jnp.sum(lp[0:1] * lp[1:2], axis=-1, keepdims=True))
                   - jnp.exp(jnp.sum(lp[2:3] * lp[3:4], axis=-1, keepdims=True)) + lam_init)
            _decode_finish(mode, acc_scr[...], l_scr[...], gate_ref, o_ref, lam, subln_ref, lam_init)
        else:
            _decode_finish(mode, acc_scr[...], l_scr[...], gate_ref, o_ref)


def _decode_attention(mode, layer, page_table, q, newkv, gate, pool_t, extra, lam_init=0.0):
    b, n_pages = page_table.shape
    pps = PAGES_PER_STEP if n_pages % PAGES_PER_STEP == 0 else n_pages
    nsteps = n_pages // pps
    r3 = lambda a: a.reshape(b, 1, a.shape[-1])
    one = lambda w: pl.BlockSpec((1, 1, w), lambda i, s, pt: (i, 0, 0))
    full = lambda a: pl.BlockSpec(a.shape, lambda i, s, pt: (0,) * a.ndim)
    if mode == "fox":
        page_of = lambda t, nd: (lambda i, s, pt: (layer, pt[i, n_pages - 1 - (s * pps + t)]) + (0,) * nd)
    else:
        page_of = lambda t, nd: (lambda i, s, pt: (layer, pt[i, s * pps + t]) + (0,) * nd)
    args = [r3(q), r3(newkv), r3(gate)]
    in_specs = [one(512), one(512), one(512)]
    if mode == "fox":
        newsmall, lft_pool = extra
        args.append(r3(newsmall)); in_specs.append(one(LANES))
    elif mode == "diff":
        lamp, subln = extra
        args += [lamp, subln]; in_specs += [full(lamp), full(subln)]
    else:
        sel, selnew = extra
        args += [sel, r3(selnew)]
        in_specs += [pl.BlockSpec((1,) + sel.shape[1:], lambda i, s, pt: (i, 0, 0)), one(LANES)]
    page_blk = (1, 1) + pool_t.shape[2:]
    for t in range(pps):
        args.append(pool_t); in_specs.append(pl.BlockSpec(page_blk, page_of(t, len(page_blk) - 2)))
    if mode == "fox":
        for t in range(pps):
            args.append(lft_pool); in_specs.append(pl.BlockSpec((1, 1, 8, PAGE), page_of(t, 2)))
    out = pl.pallas_call(
        functools.partial(_decode_attn_kernel, mode=mode, lam_init=lam_init, pps=pps),
        grid_spec=pltpu.PrefetchScalarGridSpec(
            num_scalar_prefetch=1,
            grid=(b, nsteps),
            in_specs=in_specs,
            out_specs=pl.BlockSpec((1, 1, 512), lambda i, s, pt: (i, 0, 0)),
            scratch_shapes=[pltpu.VMEM((8, 256), F32), pltpu.VMEM((8, LANES), F32), pltpu.VMEM((8, LANES), F32),
                            pltpu.VMEM((8, 256), F32), pltpu.VMEM((8, LANES), F32)],
        ),
        out_shape=jax.ShapeDtypeStruct((b, 1, 512), BF16),
        compiler_params=_cparams(("parallel", "arbitrary")),
        name="decode_attn_" + mode,
    )(page_table, *args)
    return out.reshape(b, 512)


def _idx_score_kernel(*refs, pps):
    qi_ref, wi_ref = refs[1:3]
    ki_refs = refs[3:3 + pps]
    o_ref, qh_scr, w_scr = refs[3 + pps:]
    s_idx = pl.program_id(1)

    @pl.when(s_idx == 0)
    def _():
        qrow = qi_ref[0].astype(F32)
        r = lax.broadcasted_iota(I32, (8, LANES), 0)
        c0, c1 = qrow[:, 0:LANES], qrow[:, LANES:2 * LANES]
        c0s, c1s = pltpu.roll(c0, HALF_LANES, axis=1), pltpu.roll(c1, HALF_LANES, axis=1)
        qh = jnp.where(r == 0, c0, jnp.where(r == 1, c0s, jnp.where(r == 2, c1, jnp.where(r == 3, c1s, 0.0))))
        qh_scr[...] = qh
        w_scr[...] = _lanes_to_col(wi_ref[0])

    qh = qh_scr[:, 0:HEAD_DIM].astype(BF16)
    for t in range(pps):
        sc = jnp.maximum(_dot(qh, ki_refs[t][0, 0].astype(BF16)), 0.0)
        o_ref[0, t:t + 1, :] = jnp.sum(sc * w_scr[...], axis=0, keepdims=True)


def _idx_select_kernel(sc_ref, qi_ref, wi_ref, kinew_ref, sel_ref, selnew_ref, *, topk):
    b, nkeys = sc_ref.shape
    ngrp = nkeys // LANES
    prod = qi_ref[...].astype(F32) * jnp.concatenate([kinew_ref[...], kinew_ref[...]], axis=1).astype(F32)
    lo = _lane_lo((b, LANES))
    wi = wi_ref[...]
    new = jnp.zeros((b, 1), F32)
    for h in range(IDX_HEADS):
        ch = prod[:, LANES * (h // 2):LANES * (h // 2 + 1)]
        sh = jnp.sum(jnp.where(lo == (h % 2 == 0), ch, 0.0), axis=-1, keepdims=True)
        new = new + jnp.maximum(sh, 0.0) * wi[:, h:h + 1]
    new = jnp.broadcast_to(new, (b, LANES))

    def count(pred):
        acc = jnp.zeros((b, LANES), F32)
        for g in range(ngrp):
            acc = acc + jnp.where(pred(sc_ref[:, LANES * g:LANES * (g + 1)]), 1.0, 0.0)
        return jnp.sum(acc, axis=-1, keepdims=True) + jnp.where(pred(new), 1.0, 0.0)

    thr = _kth_largest(lambda cand: count(lambda x: x >= cand), (b, LANES), topk)
    need = topk - count(lambda x: x > thr)
    r = lax.broadcasted_iota(I32, (LANES, LANES), 0)
    c = lax.broadcasted_iota(I32, (LANES, LANES), 1)
    ust = jnp.where(r < c, 1.0, 0.0).astype(BF16)
    rank = jnp.zeros((b, LANES), F32)
    for g in range(ngrp):
        gs = slice(LANES * g, LANES * (g + 1))
        x = sc_ref[:, gs]
        eq = (x == thr) & (x > -jnp.inf)
        eqf = jnp.where(eq, 1.0, 0.0)
        sel = (x > thr) | (eq & (_dot(eqf.astype(BF16), ust) + rank < need))
        sel_ref[:, gs] = jnp.where(sel, 1.0, 0.0)
        rank = rank + jnp.sum(eqf, axis=-1, keepdims=True)
    sel_new = (new > thr) | ((new == thr) & (new > -jnp.inf) & (rank < need))
    selnew_ref[...] = jnp.where(sel_new, 1.0, 0.0)


def _decode_index_select(layer, page_table, qib, wi, kib_new, idx_pool_t):
    b, n_pages = page_table.shape
    pps = 2 * PAGES_PER_STEP if n_pages % (2 * PAGES_PER_STEP) == 0 else n_pages
    nsteps = n_pages // pps
    r3 = lambda a: a.reshape(b, 1, a.shape[-1])
    one = lambda w: pl.BlockSpec((1, 1, w), lambda i, s, pt: (i, 0, 0))
    page_of = lambda t: (lambda i, s, pt: (layer, pt[i, s * pps + t], 0, 0))
    in_specs = [one(256), one(LANES)] + [pl.BlockSpec((1, 1, HEAD_DIM, PAGE), page_of(t)) for t in range(pps)]
    scores = pl.pallas_call(
        functools.partial(_idx_score_kernel, pps=pps),
        grid_spec=pltpu.PrefetchScalarGridSpec(
            num_scalar_prefetch=1,
            grid=(b, nsteps),
            in_specs=in_specs,
            out_specs=pl.BlockSpec((1, pps, PAGE), lambda i, s, pt: (i, s, 0)),
            scratch_shapes=[pltpu.VMEM((8, LANES), F32), pltpu.VMEM((8, 1), F32)],
        ),
        out_shape=jax.ShapeDtypeStruct((b, n_pages, PAGE), F32),
        compiler_params=_cparams(("parallel", "arbitrary")),
        name="decode_idx_score",
    )(page_table, r3(qib), r3(wi), *([idx_pool_t] * pps))
    nkeys = n_pages * PAGE
    full = lambda shape: pl.BlockSpec(shape, lambda i: (0,) * len(shape))
    sel, selnew = pl.pallas_call(
        functools.partial(_idx_select_kernel, topk=TOPK_MAX),
        grid=(1,),
        in_specs=[full((b, nkeys)), full((b, 256)), full((b, LANES)), full((b, LANES))],
        out_specs=[full((b, nkeys)), full((b, LANES))],
        out_shape=[jax.ShapeDtypeStruct((b, nkeys), F32), jax.ShapeDtypeStruct((b, LANES), F32)],
        compiler_params=_cparams(("arbitrary",)),
        name="decode_idx_select",
    )(scores.reshape(b, nkeys), qib, wi, kib_new)
    return sel.reshape(b, n_pages, PAGE), selnew


def _prep_even_weights(w_in, w_out):
    o = EVEN_OFF
    seg = lambda k: w_in[:, o[k]:o[k + 1]]
    small = jnp.concatenate([seg(3), seg(6), seg(7), jnp.zeros((D_MODEL, LANES - 16), F32)], axis=1)
    w = jnp.concatenate([seg(0)[:, HEAD_PERM], seg(1), seg(2), seg(4)[:, HEAD_PERM], seg(5), seg(8), small], axis=1)
    return w.astype(BF16), w_out[:512][HEAD_PERM].astype(BF16), w_out[512:].astype(BF16)


def _prep_odd_weights(w_in, w_out):
    o = ODD_OFF
    seg = lambda k: w_in[:, o[k]:o[k + 1]]
    small = jnp.concatenate([seg(5), jnp.zeros((D_MODEL, LANES - IDX_HEADS), F32)], axis=1)
    w = jnp.concatenate([seg(0)[:, HEAD_PERM], seg(1), seg(2), seg(3), seg(4), seg(4), small,
                         seg(6)[:, HEAD_PERM], seg(7), seg(8), seg(9), seg(10)], axis=1)
    return w.astype(BF16), w_out[:512][HEAD_PERM].astype(BF16), w_out[512:].astype(BF16)


def _pad_lanes(a):
    return jnp.pad(a, ((0, 0), (0, LANES - a.shape[1])))


def _kv_pool_view(cache):
    t = jnp.transpose(cache, (0, 1, 3, 4, 5, 2))
    return t.reshape(t.shape[0], t.shape[1], 2, t.shape[3] * t.shape[4], PAGE)


def _diff_pool_view(cache):
    return cache.reshape(cache.shape[0], cache.shape[1], PAGE * 4, LANES)


def _even_layer_prompt(x2d, batch, seq, wts, prm):
    w_bf, wa, wb = wts
    g_norm, f_bias, q_gain, k_gain, conv_w, a_log, dt_bias, o_gain = prm
    qb, kv, kvb, small, fg, dqkv, dg = _proj_even(x2d, g_norm, w_bf, q_gain, k_gain, f_bias, a_log, dt_bias)
    c, ct = _cumsum_logf(small, batch, seq)
    fox_o = _prompt_attention("fox", batch, seq, qb, kvb, fg, (c, ct))
    dn_o, s_fin, conv_new = _delta_prompt(batch, seq, dqkv, small, dg, conv_w, o_gain)
    y = _out_proj(x2d, fox_o, dn_o, wa, wb)
    return y, (kv, small[:, 0:8], s_fin, conv_new)


def _even_layer_decode(x2d, layer, wts, prm, past, page_table):
    w_bf, wa, wb = wts
    g_norm, f_bias, q_gain, k_gain, conv_w, a_log, dt_bias, o_gain = prm
    kv_pool_t, lft_pool, s0_all, conv_prev_all = past
    qb, kv, kvb, small, fg, dqkv, dg = _proj_even(x2d, g_norm, w_bf, q_gain, k_gain, f_bias, a_log, dt_bias)
    fox_o = _decode_attention("fox", layer, page_table, qb, kv, fg, kv_pool_t, (small, lft_pool))
    dn_o, s_fin, conv_new = _delta_decode(layer, dqkv, conv_prev_all, s0_all, small, dg, conv_w, o_gain)
    y = _out_proj(x2d, fox_o, dn_o.reshape(-1, 512), wa, wb)
    return y, (kv, small[:, 0:8], s_fin, conv_new)


def _odd_layer_prompt(x2d, batch, seq, wts, prm, tab, lam_init):
    w_bf, wa, wb = wts
    g_norm, cq_gain, ck_gain, dq_gain, dk_gain, lam_p, subln = prm
    tm = _row_tile(x2d.shape[0])
    (qb, kv, kvb, qib, kib, idx, wi, cg, dqb, dkv, dkvb, dg) = _proj_odd(
        x2d, g_norm, w_bf, tab, seq // tm, cq_gain, ck_gain, dq_gain, dk_gain)
    dsa_o = _prompt_attention("dsa", batch, seq, qb, kvb, cg, (qib, kib, wi))
    diff_o = _prompt_attention("diff", batch, seq, dqb, dkvb, dg, (_pad_lanes(lam_p), subln.reshape(1, LANES)), lam_init)
    y = _out_proj(x2d, dsa_o, diff_o, wa, wb)
    return y, (kv, idx, dkv)


def _odd_layer_decode(x2d, layer, wts, prm, tab, lam_init, past, page_table):
    w_bf, wa, wb = wts
    g_norm, cq_gain, ck_gain, dq_gain, dk_gain, lam_p, subln = prm
    dsa_pool_t, idx_pool_t, diff_pool_t = past
    (qb, kv, kvb, qib, kib, idx, wi, cg, dqb, dkv, dkvb, dg) = _proj_odd(
        x2d, g_norm, w_bf, tab, 1, cq_gain, ck_gain, dq_gain, dk_gain)
    sel, selnew = _decode_index_select(layer, page_table, qib, wi, kib, idx_pool_t)
    dsa_o = _decode_attention("dsa", layer, page_table, qb, kv, cg, dsa_pool_t, (sel, selnew))
    diff_o = _decode_attention("diff", layer, page_table, dqb, dkv, dg, diff_pool_t,
                               (_pad_lanes(lam_p), subln.reshape(1, LANES)), lam_init)
    y = _out_proj(x2d, dsa_o, diff_o, wa, wb)
    return y, (kv, idx, dkv)


def kernel(x_prompt, x_sample, cache_fox_kv, cache_fox_logf, state_delta, state_delta_conv,
           cache_dsa_kv, cache_dsa_idx_k, cache_diff_kv, page_table,
           norm_even, w_in_even, w_out_even, fox_f_bias, fox_q_norm, fox_k_norm,
           dn_conv_w, dn_a_log, dn_dt_bias, dn_out_norm,
           norm_odd, w_in_odd, w_out_odd, dsa_q_norm, dsa_k_norm, diff_q_norm, diff_k_norm,
           diff_lambda, diff_subln):
    batch, seq, _ = x_prompt.shape
    dec_b, dec_l, _ = x_sample.shape
    assert dec_l == 1
    depth = norm_even.shape[0] + norm_odd.shape[0]
    past_len = page_table.shape[1] * PAGE
    tab_p = _rope_tables(jnp.arange(seq, dtype=I32))
    tab_s = _rope_tables(jnp.full((dec_b,), past_len, I32))
    even_past = (_kv_pool_view(cache_fox_kv), jnp.swapaxes(cache_fox_logf, 2, 3), state_delta, state_delta_conv)
    odd_past = (_kv_pool_view(cache_dsa_kv), jnp.swapaxes(cache_dsa_idx_k, 2, 3), _diff_pool_view(cache_diff_kv))
    xp = x_prompt.reshape(batch * seq, D_MODEL)
    xs = x_sample.reshape(dec_b, D_MODEL)
    even_p, even_s, odd_p, odd_s = [], [], [], []
    for layer in range(depth):
        i = layer // 2
        if layer % 2 == 0:
            wts = _prep_even_weights(w_in_even[i], w_out_even[i])
            prm = (norm_even[i], fox_f_bias[i], fox_q_norm[i], fox_k_norm[i], dn_conv_w[i], dn_a_log[i],
                   dn_dt_bias[i], dn_out_norm[i])
            xp, st = _even_layer_prompt(xp, batch, seq, wts, prm)
            even_p.append(st)
            xs, st = _even_layer_decode(xs, i, wts, prm, even_past, page_table)
            even_s.append(st)
        else:
            lam_init = 0.8 - 0.6 * math.exp(-0.3 * layer)
            wts = _prep_odd_weights(w_in_odd[i], w_out_odd[i])
            prm = (norm_odd[i], dsa_q_norm[i], dsa_k_norm[i], diff_q_norm[i], diff_k_norm[i], diff_lambda[i],
                   diff_subln[i])
            xp, st = _odd_layer_prompt(xp, batch, seq, wts, prm, tab_p, lam_init)
            odd_p.append(st)
            xs, st = _odd_layer_decode(xs, i, wts, prm, tab_s, lam_init, odd_past, page_table)
            odd_s.append(st)

    def stack(states, j, shape):
        return jnp.stack([s[j] for s in states]).reshape((len(states),) + shape)

    return (xp.reshape(batch, seq, D_MODEL), xs.reshape(dec_b, 1, D_MODEL),
            stack(even_p, 0, (batch, seq, 2, 4, HEAD_DIM)), stack(even_s, 0, (dec_b, 1, 2, 4, HEAD_DIM)),
            stack(even_p, 1, (batch, seq, 8)), stack(even_s, 1, (dec_b, 1, 8)),
            stack(even_p, 2, (batch, DN_HEADS, DN_DK, LANES)), stack(even_s, 2, (dec_b, DN_HEADS, DN_DK, LANES)),
            stack(even_p, 3, (batch, CONV_W - 1, 1536)), stack(even_s, 3, (dec_b, CONV_W - 1, 1536)),
            stack(odd_p, 0, (batch, seq, 2, 4, HEAD_DIM)), stack(odd_s, 0, (dec_b, 1, 2, 4, HEAD_DIM)),
            stack(odd_p, 1, (batch, seq, HEAD_DIM)), stack(odd_s, 1, (dec_b, 1, HEAD_DIM)),
            stack(odd_p, 2, (batch, seq, 2, 2, 2 * HEAD_DIM)), stack(odd_s, 2, (dec_b, 1, 2, 2, 2 * HEAD_DIM)))
```
